```python
import math
import jax, jax.numpy as jnp
from jax import lax
import numpy as np

D_MODEL = 2048
BATCH = 1
SEQ = 16384
DEPTH = 2

N_META = 16
BLOCK = 128
LEAD_PAD = BLOCK - N_META
WINDOW = 128

ATT_HEADS = 16
ATT_KV_HEADS = 4
ATT_HEAD_DIM = 64
ATT_GROUP = ATT_HEADS // ATT_KV_HEADS
REL_BUCKETS = 32
REL_MAX_DIST = 128

RET_HEADS = 8
RET_QK_DIM = 128
RET_V_DIM = 256

D_ATT = ATT_HEADS * ATT_HEAD_DIM
D_KV = ATT_KV_HEADS * ATT_HEAD_DIM
D_RQK = RET_HEADS * RET_QK_DIM
D_RV = RET_HEADS * RET_V_DIM
D_FF = -(-8 * D_MODEL // (3 * 256)) * 256

IN_SPLITS = [D_ATT, D_KV, D_KV, D_RQK, D_RQK, D_RV, D_RV, D_MODEL, D_MODEL]
D_IN = sum(IN_SPLITS)
IN_OFFSETS = [int(o) for o in np.cumsum(IN_SPLITS)[:-1]]

NORM_EPS = 1e-5
NEG = -1e30

kernel_name = "hybrid_swa_sink_retention_gated_block"


def rmsnorm(x, gain):
    xf = x.astype(jnp.float32)
    y = xf * lax.rsqrt(jnp.mean(xf * xf, axis=-1, keepdims=True) + NORM_EPS)
    return (y * gain.astype(jnp.float32)).astype(x.dtype)


def t5_bucket(rel):
    n = jnp.maximum(rel, 0)
    max_exact = REL_BUCKETS // 2
    large = max_exact + (jnp.log(jnp.maximum(n, 1).astype(jnp.float32) / max_exact)
                         / math.log(REL_MAX_DIST / max_exact) * (REL_BUCKETS - max_exact)).astype(jnp.int32)
    large = jnp.minimum(large, REL_BUCKETS - 1)
    return jnp.where(n < max_exact, n, large)


def attention_position_terms(rel_bias, n_blocks):
    i = jnp.arange(BLOCK, dtype=jnp.int32)
    j = jnp.arange(2 * BLOCK, dtype=jnp.int32)
    blk = jnp.arange(n_blocks, dtype=jnp.int32)
    q_pos = blk[:, None] * BLOCK + i[None, :]
    k_pos = (blk[:, None] - 1) * BLOCK + j[None, :]
    rel_band = i[:, None] + BLOCK - j[None, :]
    band_valid = ((k_pos[:, None, :] >= BLOCK)
                  & (rel_band[None] >= 0) & (rel_band[None] < WINDOW))
    band_bias = rel_bias[t5_bucket(rel_band)].astype(jnp.float32)
    band_bias = band_bias.transpose(2, 0, 1).reshape(ATT_KV_HEADS, ATT_GROUP, BLOCK, 2 * BLOCK)
    meta_pos = LEAD_PAD + jnp.arange(N_META, dtype=jnp.int32)
    rel_meta = q_pos[:, :, None] - meta_pos[None, None, :]
    meta_valid = rel_meta >= 0
    meta_bias = rel_bias[t5_bucket(rel_meta)].astype(jnp.float32)
    meta_bias = meta_bias.transpose(0, 3, 1, 2).reshape(n_blocks, ATT_KV_HEADS, ATT_GROUP, BLOCK, N_META)
    return band_bias, band_valid, meta_bias, meta_valid


def sliding_window_attention(q, k, v, sinks, band_bias, band_valid, meta_bias, meta_valid):
    B, Lp = q.shape[:2]
    N = Lp // BLOCK
    dtype = q.dtype
    qb = q.reshape(B, N, BLOCK, ATT_KV_HEADS, ATT_GROUP, ATT_HEAD_DIM)
    kb = k.reshape(B, N, BLOCK, ATT_KV_HEADS, ATT_HEAD_DIM)
    vb = v.reshape(B, N, BLOCK, ATT_KV_HEADS, ATT_HEAD_DIM)
    zero = jnp.zeros_like(kb[:, :1])
    k_band = jnp.concatenate([jnp.concatenate([zero, kb[:, :-1]], axis=1), kb], axis=2)
    v_band = jnp.concatenate([jnp.concatenate([zero, vb[:, :-1]], axis=1), vb], axis=2)
    k4 = k.reshape(B, Lp, ATT_KV_HEADS, ATT_HEAD_DIM)
    v4 = v.reshape(B, Lp, ATT_KV_HEADS, ATT_HEAD_DIM)
    k_meta = k4[:, LEAD_PAD:BLOCK]
    v_meta = v4[:, LEAD_PAD:BLOCK]
    scale = ATT_HEAD_DIM ** -0.5
    s_band = jnp.einsum('bnihgd,bnjhd->bnhgij', qb, k_band).astype(jnp.float32) * scale
    s_meta = jnp.einsum('bnihgd,bmhd->bnhgim', qb, k_meta).astype(jnp.float32) * scale
    s_band = jnp.where(band_valid[None, :, None, None], s_band + band_bias, NEG)
    s_meta = jnp.where(meta_valid[None, :, None, None], s_meta + meta_bias[None], NEG)
    sink = sinks.astype(jnp.float32).reshape(1, 1, ATT_KV_HEADS, ATT_GROUP, 1, 1)
    m = jnp.maximum(jnp.maximum(s_band.max(-1, keepdims=True), s_meta.max(-1, keepdims=True)), sink)
    p_band = jnp.exp(s_band - m)
    p_meta = jnp.exp(s_meta - m)
    inv = 1.0 / (p_band.sum(-1, keepdims=True) + p_meta.sum(-1, keepdims=True) + jnp.exp(sink - m))
    o = (jnp.einsum('bnhgij,bnjhd->bnihgd', (p_band * inv).astype(dtype), v_band)
         + jnp.einsum('bnhgim,bmhd->bnihgd', (p_meta * inv).astype(dtype), v_meta))
    return o.reshape(B, Lp, D_ATT)


def rotate_every_two(x):
    x1 = x[..., ::2]
    x2 = x[..., 1::2]
    return jnp.stack([-x2, x1], axis=-1).reshape(x.shape)


def retention(q, k, v, g, valid, pos):
    B, Lp = q.shape[:2]
    N = Lp // BLOCK
    C = BLOCK
    dtype = q.dtype
    f32 = jnp.float32
    q = q.astype(f32).reshape(B, Lp, RET_HEADS, RET_QK_DIM)
    k = k.astype(f32).reshape(B, Lp, RET_HEADS, RET_QK_DIM)
    v = v.astype(f32).reshape(B, Lp, RET_HEADS, RET_V_DIM)
    angle = jnp.repeat(1.0 / (10000.0 ** jnp.linspace(0.0, 1.0, RET_QK_DIM // 2, dtype=f32)), 2)
    theta = pos.astype(f32)[:, None] * angle[None, :]
    sin = jnp.sin(theta)[None, :, None, :]
    cos = jnp.cos(theta)[None, :, None, :]
    q = q * cos + rotate_every_two(q) * sin
    k = (k * cos + rotate_every_two(k) * sin) * (RET_QK_DIM ** -0.5)
    vmask = valid[None, :, None, None]
    k = jnp.where(vmask, k, 0.0)
    v = jnp.where(vmask, v, 0.0)
    log_decay = jnp.log(1.0 - 2.0 ** (-5.0 - jnp.arange(RET_HEADS, dtype=f32)))
    qc = q.reshape(B, N, C, RET_HEADS, RET_QK_DIM)
    kc = k.reshape(B, N, C, RET_HEADS, RET_QK_DIM)
    vc = v.reshape(B, N, C, RET_HEADS, RET_V_DIM)
    idx = jnp.arange(C, dtype=f32)
    diff = idx[:, None] - idx[None, :]
    intra_decay = jnp.where(diff[None] >= 0,
                            jnp.exp(log_decay[:, None, None] * jnp.maximum(diff, 0.0)[None]), 0.0)
    s = jnp.einsum('bnihd,bnjhd->bnhij', qc, kc) * intra_decay
    o_intra = jnp.einsum('bnhij,bnjhe->bnihe', s, vc)
    k_w = kc * jnp.exp((C - 1 - idx)[:, None] * log_decay[None, :])[None, None, :, :, None]
    kv = jnp.einsum('bnjhd,bnjhe->bnhde', k_w, vc)
    chunk_decay = jnp.exp(log_decay * C)[None, :, None, None]

    def step(state, kv_n):
        return chunk_decay * state + kv_n, state

    _, s_before = lax.scan(step, jnp.zeros((B, RET_HEADS, RET_QK_DIM, RET_V_DIM), f32),
                           jnp.moveaxis(kv, 1, 0))
    s_before = jnp.moveaxis(s_before, 0, 1)
    q_w = qc * jnp.exp((idx + 1.0)[:, None] * log_decay[None, :])[None, None, :, :, None]
    o_cross = jnp.einsum('bnihd,bnhde->bnihe', q_w, s_before)
    o = (o_intra + o_cross).reshape(B, Lp, RET_HEADS, RET_V_DIM)
    o = o * lax.rsqrt(jnp.mean(o * o, axis=-1, keepdims=True) + NORM_EPS)
    o = o.reshape(B, Lp, D_RV) * jax.nn.silu(g.astype(f32))
    return o.astype(dtype)


def hybrid_layer(h, norm1, w_in, sinks, w_attn_br, w_ret_br, w_out, norm2, w_gate_up, w_down,
                 band_bias, band_valid, meta_bias, meta_valid, valid, pos):
    xn = rmsnorm(h, norm1)
    u = xn @ w_in
    aq, ak, av, rq, rk, rv, rg, gate_a, gate_r = jnp.split(u, IN_OFFSETS, axis=-1)
    a = sliding_window_attention(aq, ak, av, sinks, band_bias, band_valid, meta_bias, meta_valid) @ w_attn_br
    r = retention(rq, rk, rv, rg, valid, pos) @ w_ret_br
    merged = jax.nn.sigmoid(gate_a) * a + jax.nn.sigmoid(gate_r) * r
    h = h + merged @ w_out
    xn2 = rmsnorm(h, norm2)
    gu = xn2 @ w_gate_up
    g, up = jnp.split(gu, [D_FF], axis=-1)
    return h + (jax.nn.silu(g) * up) @ w_down


def setup_inputs(seed: int = 0) -> dict:
    key = jax.random.key(seed)
    ks = jax.random.split(key, 13)
    f32 = jnp.float32
    nrm = lambda k, shape, s: jax.random.normal(k, shape, f32) * s
    return {
        "x": nrm(ks[0], (BATCH, SEQ, D_MODEL), 1.0),
        "meta_tokens": nrm(ks[1], (N_META, D_MODEL), 1.0),
        "rel_bias": nrm(ks[2], (REL_BUCKETS, ATT_HEADS), 0.5),
        "norm1": 1.0 + nrm(ks[3], (DEPTH, D_MODEL), 0.02),
        "w_in": nrm(ks[4], (DEPTH, D_MODEL, D_IN), D_MODEL ** -0.5),
        "attn_sinks": nrm(ks[5], (DEPTH, ATT_HEADS), 1.0),
        "w_attn_br": nrm(ks[6], (DEPTH, D_ATT, D_MODEL), D_ATT ** -0.5),
        "w_ret_br": nrm(ks[7], (DEPTH, D_RV, D_MODEL), D_RV ** -0.5),
        "w_out": nrm(ks[8], (DEPTH, D_MODEL, D_MODEL), D_MODEL ** -0.5),
        "norm2": 1.0 + nrm(ks[9], (DEPTH, D_MODEL), 0.02),
        "w_gate_up": nrm(ks[10], (DEPTH, D_MODEL, 2 * D_FF), D_MODEL ** -0.5),
        "w_down": nrm(ks[11], (DEPTH, D_FF, D_MODEL), D_FF ** -0.5),
        "norm_f": 1.0 + nrm(ks[12], (D_MODEL,), 0.02),
    }


def reference(x, meta_tokens, rel_bias, norm1, w_in, attn_sinks, w_attn_br, w_ret_br, w_out,
              norm2, w_gate_up, w_down, norm_f):
    B, S, D = x.shape
    Lp = S + BLOCK
    n_blocks = Lp // BLOCK
    lead = jnp.zeros((B, LEAD_PAD, D), x.dtype)
    meta = jnp.broadcast_to(meta_tokens.astype(x.dtype)[None], (B, N_META, D))
    h = jnp.concatenate([lead, meta, x], axis=1)
    pos = jnp.arange(Lp, dtype=jnp.int32) - LEAD_PAD
    valid = pos >= 0
    band_bias, band_valid, meta_bias, meta_valid = attention_position_terms(rel_bias, n_blocks)
    for l in range(DEPTH):
        h = hybrid_layer(h, norm1[l], w_in[l], attn_sinks[l], w_attn_br[l], w_ret_br[l], w_out[l],
                         norm2[l], w_gate_up[l], w_down[l],
                         band_bias, band_valid, meta_bias, meta_valid, valid, pos)
    h = rmsnorm(h, norm_f)
    return h[:, BLOCK:]
```

```python
import math

import jax
import jax.numpy as jnp
from jax import lax
from jax.experimental import pallas as pl
from jax.experimental.pallas import tpu as pltpu

D_MODEL = 2048
SEQ = 16384
DEPTH = 2
N_META = 16
BLOCK = 128
LEAD_PAD = BLOCK - N_META
WINDOW = 128
ATT_HEADS = 16
ATT_KV_HEADS = 4
ATT_HEAD_DIM = 64
ATT_GROUP = ATT_HEADS // ATT_KV_HEADS
REL_BUCKETS = 32
REL_MAX_DIST = 128
RET_HEADS = 8
RET_QK_DIM = 128
RET_V_DIM = 256
D_ATT = ATT_HEADS * ATT_HEAD_DIM
D_KV = ATT_KV_HEADS * ATT_HEAD_DIM
D_RQK = RET_HEADS * RET_QK_DIM
D_RV = RET_HEADS * RET_V_DIM
D_FF = 5632
NORM_EPS = 1e-5
NEG = -1e30
NEG_TEST = -1e29
LP = SEQ + BLOCK
N_BLOCKS = LP // BLOCK

V7X_VMEM_BYTES = 64 * 1024 * 1024
VMEM_LIMIT = V7X_VMEM_BYTES - 8 * 1024 * 1024

TM_BIG = 1376
TM_SMALL = 688

BF16 = jnp.bfloat16
F32 = jnp.float32


def _params(*sem):
    return pltpu.CompilerParams(dimension_semantics=sem, vmem_limit_bytes=VMEM_LIMIT)


def _rms(x, gain):
    return x * lax.rsqrt(jnp.mean(x * x, axis=-1, keepdims=True) + NORM_EPS) * gain


def _norm_kernel(h_ref, g_ref, o_ref):
    o_ref[...] = _rms(h_ref[...], g_ref[...]).astype(o_ref.dtype)


def _norm(h, gain):
    return pl.pallas_call(
        _norm_kernel,
        grid=(LP // TM_SMALL,),
        in_specs=[pl.BlockSpec((TM_SMALL, D_MODEL), lambda i: (i, 0)),
                  pl.BlockSpec((1, D_MODEL), lambda i: (0, 0))],
        out_specs=pl.BlockSpec((TM_SMALL, D_MODEL), lambda i: (i, 0)),
        out_shape=jax.ShapeDtypeStruct((LP, D_MODEL), BF16),
        compiler_params=_params("parallel"),
        name="norm",
    )(h, gain.reshape(1, D_MODEL))


def _mm_kernel(a_ref, w_ref, o_ref):
    o_ref[...] = jnp.dot(a_ref[...], w_ref[...], preferred_element_type=F32).astype(o_ref.dtype)


def _matmul(a, w, tn, out_dtype, name):
    m, k = a.shape
    n = w.shape[1]
    return pl.pallas_call(
        _mm_kernel,
        grid=(m // TM_BIG, n // tn),
        in_specs=[pl.BlockSpec((TM_BIG, k), lambda i, j: (i, 0)),
                  pl.BlockSpec((k, tn), lambda i, j: (0, j))],
        out_specs=pl.BlockSpec((TM_BIG, tn), lambda i, j: (i, j)),
        out_shape=jax.ShapeDtypeStruct((m, n), out_dtype),
        compiler_params=_params("parallel", "arbitrary"),
        name=name,
    )(a, w)


def _bias_kernel(rbt_ref, bucket_ref, o_ref):
    b = bucket_ref[...]
    acc = jnp.full(o_ref.shape, NEG, F32)
    for i in range(REL_BUCKETS):
        acc = jnp.where(b == i, rbt_ref[:, i:i + 1], acc)
    o_ref[...] = acc


def _t5_bucket(rel):
    n = jnp.maximum(rel, 0)
    max_exact = REL_BUCKETS // 2
    large = max_exact + (jnp.log(jnp.maximum(n, 1).astype(F32) / max_exact)
                         / math.log(REL_MAX_DIST / max_exact) * (REL_BUCKETS - max_exact)).astype(jnp.int32)
    large = jnp.minimum(large, REL_BUCKETS - 1)
    return jnp.where(n < max_exact, n, large)


N_VARIANTS = 3


def _bias_tables(rel_bias):
    i = jnp.arange(BLOCK, dtype=jnp.int32)
    j = jnp.arange(2 * BLOCK, dtype=jnp.int32)
    blk = jnp.arange(N_VARIANTS, dtype=jnp.int32)
    q_pos = blk[:, None] * BLOCK + i[None, :]
    k_pos = (blk[:, None] - 1) * BLOCK + j[None, :]
    rel_band = i[:, None] + BLOCK - j[None, :]
    band_valid = (k_pos[:, None, :] >= BLOCK) & (rel_band[None] >= 0) & (rel_band[None] < WINDOW)
    band_bucket = jnp.where(band_valid, _t5_bucket(rel_band)[None], -1)
    meta_pos = LEAD_PAD + jnp.arange(N_META, dtype=jnp.int32)
    rel_meta = q_pos[:, :, None] - meta_pos[None, None, :]
    meta_bucket = jnp.where(rel_meta >= 0, _t5_bucket(rel_meta), -1)
    n_band = N_VARIANTS * BLOCK * 2 * BLOCK
    bucket = jnp.concatenate([band_bucket.reshape(-1), meta_bucket.reshape(-1)])[None, :]
    p_total = bucket.shape[1]
    n_chunks = 8
    pc = p_total // n_chunks
    out = pl.pallas_call(
        _bias_kernel,
        grid=(n_chunks,),
        in_specs=[pl.BlockSpec((ATT_HEADS, REL_BUCKETS), lambda c: (0, 0)),
                  pl.BlockSpec((1, pc), lambda c: (0, c))],
        out_specs=pl.BlockSpec((ATT_HEADS, pc), lambda c: (0, c)),
        out_shape=jax.ShapeDtypeStruct((ATT_HEADS, p_total), F32),
        compiler_params=_params("parallel"),
        name="bias_tables",
    )(rel_bias.T, bucket)
    band = out[:, :n_band].reshape(ATT_HEADS, N_VARIANTS, BLOCK, 2 * BLOCK).transpose(1, 0, 2, 3)
    meta = out[:, n_band:].reshape(ATT_HEADS, N_VARIANTS, BLOCK, N_META).transpose(1, 0, 2, 3)
    return band, meta


_NT = (((1,), (1,)), ((), ()))


def _attn_kernel(sink_ref, q_ref, kvc_ref, kvp_ref, kvm_ref, bband_ref, bmeta_ref, o_ref):
    scale = ATT_HEAD_DIM ** -0.5
    q = q_ref[...] * jnp.asarray(scale, BF16)
    dh = ATT_HEAD_DIM
    for h in range(ATT_KV_HEADS):
        k_band = jnp.concatenate([kvp_ref[:, h * dh:(h + 1) * dh], kvc_ref[:, h * dh:(h + 1) * dh]], axis=0)
        v_band = jnp.concatenate([kvp_ref[:, D_KV + h * dh:D_KV + (h + 1) * dh],
                                  kvc_ref[:, D_KV + h * dh:D_KV + (h + 1) * dh]], axis=0)
        k_meta = kvm_ref[LEAD_PAD:BLOCK, h * dh:(h + 1) * dh]
        v_meta = kvm_ref[LEAD_PAD:BLOCK, D_KV + h * dh:D_KV + (h + 1) * dh]
        for g in range(ATT_GROUP):
            hq = h * ATT_GROUP + g
            qh = q[:, hq * dh:(hq + 1) * dh]
            s_band = lax.dot_general(qh, k_band, _NT, preferred_element_type=F32)
            s_meta = lax.dot_general(qh, k_meta, _NT, preferred_element_type=F32)
            tb = bband_ref[0, hq]
            tm = bmeta_ref[0, hq]
            s_band = jnp.where(tb > NEG_TEST, s_band + tb, NEG)
            s_meta = jnp.where(tm > NEG_TEST, s_meta + tm, NEG)
            sink = sink_ref[hq]
            m = jnp.maximum(jnp.maximum(jnp.max(s_band, axis=-1, keepdims=True),
                                        jnp.max(s_meta, axis=-1, keepdims=True)), sink)
            p_band = jnp.exp(s_band - m)
            p_meta = jnp.exp(s_meta - m)
            inv = 1.0 / (jnp.sum(p_band, axis=-1, keepdims=True) + jnp.sum(p_meta, axis=-1, keepdims=True)
                         + jnp.exp(sink - m))
            o = (jnp.dot((p_band * inv).astype(BF16), v_band, preferred_element_type=F32)
                 + jnp.dot((p_meta * inv).astype(BF16), v_meta, preferred_element_type=F32))
            o_ref[:, hq * dh:(hq + 1) * dh] = o.astype(o_ref.dtype)


def _attention(ubf, sinks, bias_band, bias_meta):
    kv_col = D_ATT // (2 * D_KV)
    variant = lambda n: jnp.minimum(n, N_VARIANTS - 1)
    return pl.pallas_call(
        _attn_kernel,
        grid=(N_BLOCKS,),
        in_specs=[pl.BlockSpec(memory_space=pltpu.SMEM),
                  pl.BlockSpec((BLOCK, D_ATT), lambda n: (n, 0)),
                  pl.BlockSpec((BLOCK, 2 * D_KV), lambda n: (n, kv_col)),
                  pl.BlockSpec((BLOCK, 2 * D_KV), lambda n: (jnp.maximum(n - 1, 0), kv_col)),
                  pl.BlockSpec((BLOCK, 2 * D_KV), lambda n: (0, kv_col)),
                  pl.BlockSpec((1, ATT_HEADS, BLOCK, 2 * BLOCK), lambda n: (variant(n), 0, 0, 0)),
                  pl.BlockSpec((1, ATT_HEADS, BLOCK, N_META), lambda n: (variant(n), 0, 0, 0))],
        out_specs=pl.BlockSpec((BLOCK, D_ATT), lambda n: (n, 0)),
        out_shape=jax.ShapeDtypeStruct((LP, D_ATT), BF16),
        compiler_params=_params("parallel"),
        name="attention",
    )(sinks, ubf, ubf, ubf, ubf, bias_band, bias_meta)


_TN = (((0,), (0,)), ((), ()))


def _ret_kernel(cdec_ref, q_ref, k_ref, v_ref, g_ref, cos_ref, sina_ref, sinb_ref,
                dmask_ref, qdec_ref, kdec_ref, o_ref, state_ref):
    n = pl.program_id(0)

    @pl.when(n == 0)
    def _():
        state_ref[...] = jnp.zeros_like(state_ref)

    row = lax.broadcasted_iota(jnp.int32, (BLOCK, 1), 0)
    valid = (n * BLOCK + row) >= LEAD_PAD
    cos = cos_ref[...]
    sina = sina_ref[...]
    sinb = sinb_ref[...]
    dk, dv = RET_QK_DIM, RET_V_DIM
    scale = RET_QK_DIM ** -0.5

    def rotary(x):
        return x * cos + pltpu.roll(x, dk - 1, 1) * sina + pltpu.roll(x, 1, 1) * sinb

    for h in range(RET_HEADS):
        q = rotary(q_ref[:, h * dk:(h + 1) * dk])
        k = jnp.where(valid, rotary(k_ref[:, h * dk:(h + 1) * dk]) * scale, 0.0)
        v = v_ref[:, h * dv:(h + 1) * dv]
        v = jnp.where(valid, v, jnp.zeros_like(v))
        s = lax.dot_general(q.astype(BF16), k.astype(BF16), _NT, preferred_element_type=F32) * dmask_ref[h]
        o = jnp.dot(s.astype(BF16), v, preferred_element_type=F32)
        state = state_ref[h]
        o = o + jnp.dot((q * qdec_ref[h]).astype(BF16), state.astype(BF16), preferred_element_type=F32)
        kw = (k * kdec_ref[h]).astype(BF16)
        state_ref[h] = cdec_ref[h] * state + lax.dot_general(kw, v, _TN, preferred_element_type=F32)
        o = o * lax.rsqrt(jnp.mean(o * o, axis=-1, keepdims=True) + NORM_EPS)
        g = g_ref[:, h * dv:(h + 1) * dv]
        o = o * (g * (1.0 / (1.0 + jnp.exp(-g))))
        o_ref[:, h * dv:(h + 1) * dv] = o.astype(o_ref.dtype)


def _retention_tables():
    f32 = F32
    pos = (jnp.arange(LP, dtype=jnp.int32) - LEAD_PAD).astype(f32)
    angle = jnp.repeat(1.0 / (10000.0 ** jnp.linspace(0.0, 1.0, RET_QK_DIM // 2, dtype=f32)), 2)
    theta = pos[:, None] * angle[None, :]
    sin = jnp.sin(theta)
    cos = jnp.cos(theta)
    even = (jnp.arange(RET_QK_DIM) % 2 == 0)[None, :]
    sina = jnp.where(even, -sin, 0.0)
    sinb = jnp.where(even, 0.0, sin)
    log_decay = jnp.log(1.0 - 2.0 ** (-5.0 - jnp.arange(RET_HEADS, dtype=f32)))
    idx = jnp.arange(BLOCK, dtype=f32)
    diff = idx[:, None] - idx[None, :]
    dmask = jnp.where(diff[None] >= 0, jnp.exp(log_decay[:, None, None] * jnp.maximum(diff, 0.0)[None]), 0.0)
    kdec = jnp.exp((BLOCK - 1 - idx)[:, None] * log_decay[None, :])
    qdec = jnp.exp((idx + 1.0)[:, None] * log_decay[None, :])
    kdec = jnp.broadcast_to(kdec.T[:, :, None], (RET_HEADS, BLOCK, RET_QK_DIM))
    qdec = jnp.broadcast_to(qdec.T[:, :, None], (RET_HEADS, BLOCK, RET_QK_DIM))
    cdec = jnp.exp(log_decay * BLOCK)
    return cos, sina, sinb, dmask, qdec, kdec, cdec


def _retention(uf, rv, tables):
    cos, sina, sinb, dmask, qdec, kdec, cdec = tables
    rq_rk_src = rg_src = uf
    full = lambda shape: pl.BlockSpec(shape, lambda n: (0,) * len(shape))
    return pl.pallas_call(
        _ret_kernel,
        grid=(N_BLOCKS,),
        in_specs=[pl.BlockSpec(memory_space=pltpu.SMEM),
                  pl.BlockSpec((BLOCK, D_RQK), lambda n: (n, 0)),
                  pl.BlockSpec((BLOCK, D_RQK), lambda n: (n, 1)),
                  pl.BlockSpec((BLOCK, D_RV), lambda n: (n, 0)),
                  pl.BlockSpec((BLOCK, D_RV), lambda n: (n, 1)),
                  pl.BlockSpec((BLOCK, RET_QK_DIM), lambda n: (n, 0)),
                  pl.BlockSpec((BLOCK, RET_QK_DIM), lambda n: (n, 0)),
                  pl.BlockSpec((BLOCK, RET_QK_DIM), lambda n: (n, 0)),
                  full((RET_HEADS, BLOCK, BLOCK)),
                  full((RET_HEADS, BLOCK, RET_QK_DIM)),
                  full((RET_HEADS, BLOCK, RET_QK_DIM))],
        out_specs=pl.BlockSpec((BLOCK, D_RV), lambda n: (n, 0)),
        out_shape=jax.ShapeDtypeStruct((LP, D_RV), BF16),
        scratch_shapes=[pltpu.VMEM((RET_HEADS, RET_QK_DIM, RET_V_DIM), F32)],
        compiler_params=_params("arbitrary"),
        name="retention",
    )(cdec, rq_rk_src, rq_rk_src, rv, rg_src, cos, sina, sinb, dmask, qdec, kdec)


def _branch_kernel(a_ref, r_ref, wa_ref, wr_ref, ga_ref, gr_ref, o_ref):
    a = jnp.dot(a_ref[...], wa_ref[...], preferred_element_type=F32)
    r = jnp.dot(r_ref[...], wr_ref[...], preferred_element_type=F32)
    o_ref[...] = (jax.nn.sigmoid(ga_ref[...]) * a + jax.nn.sigmoid(gr_ref[...]) * r).astype(o_ref.dtype)


def _branch_merge(att_o, ret_o, wa, wr, uf, ga_col, gr_col):
    tn = 512
    return pl.pallas_call(
        _branch_kernel,
        grid=(LP // TM_BIG, D_MODEL // tn),
        in_specs=[pl.BlockSpec((TM_BIG, D_ATT), lambda i, j: (i, 0)),
                  pl.BlockSpec((TM_BIG, D_RV), lambda i, j: (i, 0)),
                  pl.BlockSpec((D_ATT, tn), lambda i, j: (0, j)),
                  pl.BlockSpec((D_RV, tn), lambda i, j: (0, j)),
                  pl.BlockSpec((TM_BIG, tn), lambda i, j: (i, ga_col // tn + j)),
                  pl.BlockSpec((TM_BIG, tn), lambda i, j: (i, gr_col // tn + j))],
        out_specs=pl.BlockSpec((TM_BIG, tn), lambda i, j: (i, j)),
        out_shape=jax.ShapeDtypeStruct((LP, D_MODEL), BF16),
        compiler_params=_params("parallel", "arbitrary"),
        name="branch_merge",
    )(att_o, ret_o, wa, wr, uf, uf)


def _out_kernel(a_ref, w_ref, h_ref, g_ref, ho_ref, xn_ref):
    h = h_ref[...] + jnp.dot(a_ref[...], w_ref[...], preferred_element_type=F32)
    ho_ref[...] = h
    xn_ref[...] = _rms(h, g_ref[...]).astype(xn_ref.dtype)


def _out_proj(merged, wo, h, gain):
    return pl.pallas_call(
        _out_kernel,
        grid=(LP // TM_SMALL,),
        in_specs=[pl.BlockSpec((TM_SMALL, D_MODEL), lambda i: (i, 0)),
                  pl.BlockSpec((D_MODEL, D_MODEL), lambda i: (0, 0), pipeline_mode=pl.Buffered(1)),
                  pl.BlockSpec((TM_SMALL, D_MODEL), lambda i: (i, 0)),
                  pl.BlockSpec((1, D_MODEL), lambda i: (0, 0))],
        out_specs=[pl.BlockSpec((TM_SMALL, D_MODEL), lambda i: (i, 0)),
                   pl.BlockSpec((TM_SMALL, D_MODEL), lambda i: (i, 0))],
        out_shape=[jax.ShapeDtypeStruct((LP, D_MODEL), F32),
                   jax.ShapeDtypeStruct((LP, D_MODEL), BF16)],
        compiler_params=_params("parallel"),
        name="out_proj",
    )(merged, wo, h, gain.reshape(1, D_MODEL))


def _gate_up_kernel(x_ref, wg_ref, wu_ref, o_ref):
    x = x_ref[...]
    g = jnp.dot(x, wg_ref[...], preferred_element_type=F32)
    u = jnp.dot(x, wu_ref[...], preferred_element_type=F32)
    o_ref[...] = (g * jax.nn.sigmoid(g) * u).astype(o_ref.dtype)


def _gate_up(xn, wgu):
    tn = 512
    n_tiles = D_FF // tn
    return pl.pallas_call(
        _gate_up_kernel,
        grid=(LP // TM_BIG, n_tiles),
        in_specs=[pl.BlockSpec((TM_BIG, D_MODEL), lambda i, j: (i, 0)),
                  pl.BlockSpec((D_MODEL, tn), lambda i, j: (0, j)),
                  pl.BlockSpec((D_MODEL, tn), lambda i, j: (0, n_tiles + j))],
        out_specs=pl.BlockSpec((TM_BIG, tn), lambda i, j: (i, j)),
        out_shape=jax.ShapeDtypeStruct((LP, D_FF), BF16),
        compiler_params=_params("parallel", "arbitrary"),
        name="gate_up",
    )(xn, wgu, wgu)


def _down_kernel(a_ref, w_ref, h_ref, g_ref, ho_ref, xn_ref):
    k = pl.program_id(1)
    part = jnp.dot(a_ref[...], w_ref[...], preferred_element_type=F32)

    @pl.when(k == 0)
    def _():
        ho_ref[...] = h_ref[...] + part

    @pl.when(k > 0)
    def _():
        ho_ref[...] += part

    @pl.when(k == pl.num_programs(1) - 1)
    def _():
        xn_ref[...] = _rms(ho_ref[...], g_ref[...]).astype(xn_ref.dtype)


def _down_proj(act, wd, h, gain, xn_dtype):
    tk = D_FF // 4
    return pl.pallas_call(
        _down_kernel,
        grid=(LP // TM_SMALL, D_FF // tk),
        in_specs=[pl.BlockSpec((TM_SMALL, tk), lambda i, k: (i, k)),
                  pl.BlockSpec((tk, D_MODEL), lambda i, k: (k, 0)),
                  pl.BlockSpec((TM_SMALL, D_MODEL), lambda i, k: (i, 0)),
                  pl.BlockSpec((1, D_MODEL), lambda i, k: (0, 0))],
        out_specs=[pl.BlockSpec((TM_SMALL, D_MODEL), lambda i, k: (i, 0)),
                   pl.BlockSpec((TM_SMALL, D_MODEL), lambda i, k: (i, 0))],
        out_shape=[jax.ShapeDtypeStruct((LP, D_MODEL), F32),
                   jax.ShapeDtypeStruct((LP, D_MODEL), xn_dtype)],
        compiler_params=_params("parallel", "arbitrary"),
        name="down_proj",
    )(act, wd, h, gain.reshape(1, D_MODEL))


_OFF = [0, D_ATT, D_ATT + D_KV, D_ATT + 2 * D_KV, D_ATT + 2 * D_KV + D_RQK, D_ATT + 2 * D_KV + 2 * D_RQK]
_RV0 = _OFF[5]
_RG0 = _RV0 + D_RV
_GA0 = _RG0 + D_RV
_GR0 = _GA0 + D_MODEL
_END = _GR0 + D_MODEL


def kernel(x, meta_tokens, rel_bias, norm1, w_in, attn_sinks, w_attn_br, w_ret_br, w_out,
           norm2, w_gate_up, w_down, norm_f):
    assert x.shape == (1, SEQ, D_MODEL)
    lead = jnp.zeros((LEAD_PAD, D_MODEL), x.dtype)
    h = jnp.concatenate([lead, meta_tokens.astype(x.dtype), x[0]], axis=0)

    bias_band, bias_meta = _bias_tables(rel_bias)
    tables = _retention_tables()

    xn = _norm(h, norm1[0])
    for l in range(DEPTH):
        wl = w_in[l]
        w_qkv = wl[:, :_OFF[3]].astype(BF16)
        w_rv = wl[:, _RV0:_RG0].astype(BF16)
        w_f = jnp.concatenate([wl[:, _OFF[3]:_RV0], wl[:, _RG0:_END]], axis=1).astype(BF16)
        u_qkv = _matmul(xn, w_qkv, 768, BF16, "in_qkv")
        u_rv = _matmul(xn, w_rv, 1024, BF16, "in_rv")
        uf = _matmul(xn, w_f, 1024, F32, "in_f32")
        att_o = _attention(u_qkv, attn_sinks[l], bias_band, bias_meta)
        ret_o = _retention(uf, u_rv, tables)
        merged = _branch_merge(att_o, ret_o, w_attn_br[l].astype(BF16), w_ret_br[l].astype(BF16),
                               uf, 2 * D_RQK + D_RV, 2 * D_RQK + D_RV + D_MODEL)
        h, xn2 = _out_proj(merged, w_out[l].astype(BF16), h, norm2[l])
        act = _gate_up(xn2, w_gate_up[l].astype(BF16))
        last = l == DEPTH - 1
        gain = norm_f if last else norm1[l + 1]
        h, xn = _down_proj(act, w_down[l].astype(BF16), h, gain, F32 if last else BF16)
    return xn[BLOCK:][None]
```
